```python
import math
import jax, jax.numpy as jnp
from jax import lax
import numpy as np


D_MODEL = 4096
BATCH = 4
SEQ = 4096
DEPTH = 1

N_META = 16
ATT_HEADS = D_MODEL // 256
HEAD_DIM = 128
ATT_WIDTH = ATT_HEADS * HEAD_DIM
Q_BLOCK = 128
SSM_GROUP_CH = 16
SSM_GROUPS = D_MODEL // 32
SSM_WIDTH = SSM_GROUPS * SSM_GROUP_CH
SSM_STATE = 64
PEER_HEADS = 8
PEER_KEYS = 128
PEER_N = PEER_KEYS * PEER_KEYS
PEER_TOPK = 16
PEER_KEY_DIM = 256
PEER_HALF = PEER_KEY_DIM // 2
PEER_CHUNK = 16
EPS = 1e-6

COL_Q = 0
COL_K = COL_Q + ATT_WIDTH
COL_V = COL_K + ATT_WIDTH
COL_F = COL_V + ATT_WIDTH
COL_U = COL_F + ATT_HEADS
COL_GA = COL_U + SSM_WIDTH
COL_GB = COL_GA + D_MODEL
N_COLS = COL_GB + D_MODEL

kernel_name = 'hybrid_fox_s5_peer_block'


def rmsnorm(x, g):
    xf = x.astype(jnp.float32)
    y = xf * lax.rsqrt(jnp.mean(xf * xf, axis=-1, keepdims=True) + EPS) * g.astype(jnp.float32)
    return y.astype(x.dtype)


def forgetting_attention(q, k, v, logf):
    L = q.shape[1]
    scale = 1.0 / math.sqrt(HEAD_DIM)
    c = jnp.swapaxes(jnp.cumsum(logf, axis=1), 1, 2)
    n_blocks = (L - N_META) // Q_BLOCK
    bounds = [(0, N_META)] + [(N_META + i * Q_BLOCK, N_META + (i + 1) * Q_BLOCK) for i in range(n_blocks)]
    outs = []
    for s0, s1 in bounds:
        qb = q[:, s0:s1]
        kb = k[:, :s1]
        vb = v[:, :s1]
        logits = jnp.einsum('bqhd,bkhd->bhqk', qb, kb, preferred_element_type=jnp.float32) * scale
        logits = logits + (c[:, :, s0:s1, None] - c[:, :, None, :s1])
        mask = (s0 + jnp.arange(s1 - s0))[:, None] >= jnp.arange(s1)[None, :]
        logits = jnp.where(mask, logits, -jnp.inf)
        p = jax.nn.softmax(logits, axis=-1)
        outs.append(jnp.einsum('bhqk,bkhd->bqhd', p.astype(vb.dtype), vb))
    return jnp.concatenate(outs, axis=1)


def s5_branch(u, lam_re, lam_im, log_dt, b_re, b_im, c_re, c_im, d_skip, w_glu):
    Bb, L, _ = u.shape
    f32 = jnp.float32
    uf = u.astype(f32).reshape(Bb, L, SSM_GROUPS, SSM_GROUP_CH)
    dt = jnp.exp(log_dt.astype(f32))[:, None]
    lr = lam_re.astype(f32)
    li = lam_im.astype(f32)
    mag = jnp.exp(lr * dt)
    ar = mag * jnp.cos(li * dt)
    ai = mag * jnp.sin(li * dt)
    nr = ar - 1.0
    den = lr * lr + li * li
    fr = (nr * lr + ai * li) / den
    fi = (ai * lr - nr * li) / den
    br = b_re.astype(f32)
    bi = b_im.astype(f32)
    bbr = fr[..., None] * br - fi[..., None] * bi
    bbi = fr[..., None] * bi + fi[..., None] * br
    xr = jnp.einsum('blgc,gpc->blgp', uf, bbr)
    xi = jnp.einsum('blgc,gpc->blgp', uf, bbi)
    a_r = jnp.broadcast_to(ar, (1, L, SSM_GROUPS, SSM_STATE))
    a_i = jnp.broadcast_to(ai, (1, L, SSM_GROUPS, SSM_STATE))

    def combine(e1, e2):
        a1r, a1i, b1r, b1i = e1
        a2r, a2i, b2r, b2i = e2
        return (a1r * a2r - a1i * a2i,
                a1r * a2i + a1i * a2r,
                a2r * b1r - a2i * b1i + b2r,
                a2r * b1i + a2i * b1r + b2i)

    _, _, sr, si = lax.associative_scan(combine, (a_r, a_i, xr, xi), axis=1)
    y = (jnp.einsum('blgp,gcp->blgc', sr, c_re.astype(f32))
         - jnp.einsum('blgp,gcp->blgc', si, c_im.astype(f32))
         + d_skip.astype(f32) * uf)
    z = jax.nn.gelu(y.reshape(Bb, L, SSM_WIDTH))
    out = z * jax.nn.sigmoid(z @ w_glu.astype(f32))
    return out.astype(u.dtype)


def token_mixing(hn, w_in, b_forget, q_norm_g, k_norm_g, lam_re, lam_im, log_dt,
                 b_re, b_im, c_re, c_im, d_skip, w_glu, w_branch_attn, w_branch_ssm, w_out):
    Bb, L, _ = hn.shape
    proj = hn @ w_in
    q = rmsnorm(proj[..., COL_Q:COL_K].reshape(Bb, L, ATT_HEADS, HEAD_DIM), q_norm_g)
    k = rmsnorm(proj[..., COL_K:COL_V].reshape(Bb, L, ATT_HEADS, HEAD_DIM), k_norm_g)
    v = proj[..., COL_V:COL_F].reshape(Bb, L, ATT_HEADS, HEAD_DIM)
    logf = jax.nn.log_sigmoid((proj[..., COL_F:COL_U] + b_forget).astype(jnp.float32))
    attn = forgetting_attention(q, k, v, logf).reshape(Bb, L, ATT_WIDTH)
    ssm = s5_branch(proj[..., COL_U:COL_GA], lam_re, lam_im, log_dt, b_re, b_im,
                    c_re, c_im, d_skip, w_glu)
    g_a = jax.nn.sigmoid(proj[..., COL_GA:COL_GB])
    g_b = jax.nn.sigmoid(proj[..., COL_GB:N_COLS])
    mix = g_a * (attn @ w_branch_attn) + g_b * (ssm @ w_branch_ssm)
    return mix @ w_out


def peer_ffn(hn, w_query, sub_keys, expert_u, expert_v):
    Bb, L, D = hn.shape
    T = Bb * L
    xt = hn.reshape(T, D)
    q = (xt @ w_query).reshape(T, PEER_HEADS, 2, PEER_HALF)
    s = jnp.einsum('thcd,cnd->thcn', q, sub_keys, preferred_element_type=jnp.float32)
    sv, si = lax.top_k(s, PEER_TOPK)
    cand = (sv[:, :, 0, :, None] + sv[:, :, 1, None, :]).reshape(T, PEER_HEADS, PEER_TOPK * PEER_TOPK)
    best, pos = lax.top_k(cand, PEER_TOPK)
    i1 = jnp.take_along_axis(si[:, :, 0], pos // PEER_TOPK, axis=-1)
    i2 = jnp.take_along_axis(si[:, :, 1], pos % PEER_TOPK, axis=-1)
    experts = i1 * PEER_KEYS + i2
    gates = jax.nn.softmax(best, axis=-1)
    n_chunks = T // PEER_CHUNK

    def chunk(args):
        xc, ec, gc = args
        u = jnp.take(expert_u, ec, axis=0)
        a = jnp.einsum('cd,chkd->chk', xc, u, preferred_element_type=jnp.float32)
        w = (gc * jax.nn.gelu(a)).astype(xc.dtype)
        vv = jnp.take(expert_v, ec, axis=0)
        return jnp.einsum('chk,chkd->cd', w, vv)

    y = lax.map(chunk, (xt.reshape(n_chunks, PEER_CHUNK, D),
                        experts.reshape(n_chunks, PEER_CHUNK, PEER_HEADS, PEER_TOPK),
                        gates.reshape(n_chunks, PEER_CHUNK, PEER_HEADS, PEER_TOPK)))
    return y.reshape(Bb, L, D)


def setup_inputs(seed: int = 0) -> dict:
    key = jax.random.key(seed)
    ks = jax.random.split(key, 24)
    f32 = jnp.float32
    nrm = lambda k, shp, s: jax.random.normal(k, shp, f32) * s
    lam_im_base = math.pi * jnp.arange(SSM_STATE, dtype=f32)
    return {
        'x': nrm(ks[0], (BATCH, SEQ, D_MODEL), 1.0),
        'meta_tokens': nrm(ks[1], (N_META, D_MODEL), 1.0),
        'norm1_g': 1.0 + nrm(ks[2], (DEPTH, D_MODEL), 0.02),
        'w_in': nrm(ks[3], (DEPTH, D_MODEL, N_COLS), D_MODEL ** -0.5),
        'b_forget': 3.0 + nrm(ks[4], (DEPTH, ATT_HEADS), 0.5),
        'q_norm_g': 1.0 + nrm(ks[5], (DEPTH, HEAD_DIM), 0.02),
        'k_norm_g': 1.0 + nrm(ks[6], (DEPTH, HEAD_DIM), 0.02),
        'lam_re': -0.5 + nrm(ks[7], (DEPTH, SSM_GROUPS, SSM_STATE), 0.01),
        'lam_im': lam_im_base + nrm(ks[8], (DEPTH, SSM_GROUPS, SSM_STATE), 0.01),
        'log_dt': jax.random.uniform(ks[9], (DEPTH, SSM_GROUPS), f32, math.log(1e-3), math.log(1e-1)),
        'b_re': nrm(ks[10], (DEPTH, SSM_GROUPS, SSM_STATE, SSM_GROUP_CH), (2 * SSM_GROUP_CH) ** -0.5),
        'b_im': nrm(ks[11], (DEPTH, SSM_GROUPS, SSM_STATE, SSM_GROUP_CH), (2 * SSM_GROUP_CH) ** -0.5),
        'c_re': nrm(ks[12], (DEPTH, SSM_GROUPS, SSM_GROUP_CH, SSM_STATE), SSM_STATE ** -0.5),
        'c_im': nrm(ks[13], (DEPTH, SSM_GROUPS, SSM_GROUP_CH, SSM_STATE), SSM_STATE ** -0.5),
        'd_skip': 1.0 + nrm(ks[14], (DEPTH, SSM_GROUPS, SSM_GROUP_CH), 0.1),
        'w_glu': nrm(ks[15], (DEPTH, SSM_WIDTH, SSM_WIDTH), SSM_WIDTH ** -0.5),
        'w_branch_attn': nrm(ks[16], (DEPTH, ATT_WIDTH, D_MODEL), ATT_WIDTH ** -0.5),
        'w_branch_ssm': nrm(ks[17], (DEPTH, SSM_WIDTH, D_MODEL), SSM_WIDTH ** -0.5),
        'w_out': nrm(ks[18], (DEPTH, D_MODEL, D_MODEL), D_MODEL ** -0.5),
        'norm2_g': 1.0 + nrm(ks[19], (DEPTH, D_MODEL), 0.02),
        'w_query': nrm(ks[20], (DEPTH, D_MODEL, PEER_HEADS * PEER_KEY_DIM), D_MODEL ** -0.5),
        'sub_keys': nrm(ks[21], (DEPTH, 2, PEER_KEYS, PEER_HALF), PEER_HALF ** -0.5),
        'expert_u': nrm(ks[22], (DEPTH, PEER_N, D_MODEL), D_MODEL ** -0.5),
        'expert_v': nrm(ks[23], (DEPTH, PEER_N, D_MODEL), (PEER_HEADS * PEER_TOPK) ** -0.5),
    }


def reference(x, meta_tokens, norm1_g, w_in, b_forget, q_norm_g, k_norm_g, lam_re, lam_im,
              log_dt, b_re, b_im, c_re, c_im, d_skip, w_glu, w_branch_attn, w_branch_ssm,
              w_out, norm2_g, w_query, sub_keys, expert_u, expert_v):
    Bb = x.shape[0]
    meta = jnp.broadcast_to(meta_tokens[None].astype(x.dtype), (Bb, N_META, x.shape[-1]))
    h = jnp.concatenate([meta, x], axis=1)
    for layer in range(DEPTH):
        h = h + token_mixing(rmsnorm(h, norm1_g[layer]), w_in[layer], b_forget[layer],
                             q_norm_g[layer], k_norm_g[layer], lam_re[layer], lam_im[layer],
                             log_dt[layer], b_re[layer], b_im[layer], c_re[layer], c_im[layer],
                             d_skip[layer], w_glu[layer], w_branch_attn[layer],
                             w_branch_ssm[layer], w_out[layer])
        h = h + peer_ffn(rmsnorm(h, norm2_g[layer]), w_query[layer], sub_keys[layer],
                         expert_u[layer], expert_v[layer])
    return h[:, N_META:]
```

```python
import functools
import math

import jax
import jax.numpy as jnp
import numpy as np
from jax import lax
from jax.experimental import pallas as pl
from jax.experimental.pallas import tpu as pltpu

F32 = jnp.float32
BF16 = jnp.bfloat16

EPS = 1e-6
N_META = 16
HEAD_DIM = 128
SSM_GROUP_CH = 16
SSM_STATE = 64
SSM_CHUNK = 16
SSM_BATCH_PAD = 8
PEER_HEADS = 8
PEER_KEYS = 128
PEER_TOPK = 16
PEER_HALF = 128
LANES = 128
NEG_BIG = -1e30
VMEM_LIMIT = 48 * 1024 * 1024


def _cparams(sem, vmem=VMEM_LIMIT):
    return pltpu.CompilerParams(dimension_semantics=sem, vmem_limit_bytes=vmem)


def _gelu_tanh(x):
    return 0.5 * x * (1.0 + jnp.tanh(0.7978845608028654 * (x + 0.044715 * (x * x * x))))


def _sigmoid(x):
    return 1.0 / (1.0 + jnp.exp(-x))


def _log_sigmoid(x):
    return jnp.minimum(x, 0.0) - jnp.log(1.0 + jnp.exp(-jnp.abs(x)))


def _rmsnorm_body(x_ref, g_ref, o_ref):
    x = x_ref[...]
    ms = jnp.mean(x * x, axis=-1, keepdims=True)
    o_ref[...] = (x * lax.rsqrt(ms + EPS) * g_ref[...]).astype(o_ref.dtype)


def _rmsnorm(x2d, g, tm):
    m, d = x2d.shape
    return pl.pallas_call(
        _rmsnorm_body,
        grid=(m // tm,),
        in_specs=[pl.BlockSpec((tm, d), lambda i: (i, 0)),
                  pl.BlockSpec((1, d), lambda i: (0, 0))],
        out_specs=pl.BlockSpec((tm, d), lambda i: (i, 0)),
        out_shape=jax.ShapeDtypeStruct((m, d), BF16),
        compiler_params=_cparams(("parallel",)),
        name="rmsnorm",
    )(x2d, g.reshape(1, d).astype(F32))


def _mm_body(*refs, mode):
    if mode == "mix":
        a1_ref, w1_ref, a2_ref, w2_ref, ga_ref, gb_ref, o_ref = refs
        d1 = jnp.dot(a1_ref[...], w1_ref[...], preferred_element_type=F32)
        d2 = jnp.dot(a2_ref[...], w2_ref[...], preferred_element_type=F32)
        o_ref[...] = (ga_ref[...].astype(F32) * d1 + gb_ref[...].astype(F32) * d2).astype(o_ref.dtype)
        return
    a_ref, w_ref = refs[0], refs[1]
    o_ref = refs[-1]
    acc = jnp.dot(a_ref[...], w_ref[...], preferred_element_type=F32)
    if mode == "plain":
        o_ref[...] = acc.astype(o_ref.dtype)
    elif mode == "sigmoid":
        o_ref[...] = _sigmoid(acc).astype(o_ref.dtype)
    elif mode == "logsig_bias":
        o_ref[...] = _log_sigmoid(acc + refs[2][...]).astype(o_ref.dtype)
    elif mode == "headnorm":
        g_ref = refs[2]
        for hh in range(acc.shape[1] // HEAD_DIM):
            sl = slice(hh * HEAD_DIM, (hh + 1) * HEAD_DIM)
            blk = acc[:, sl]
            ms = jnp.mean(blk * blk, axis=-1, keepdims=True)
            o_ref[:, sl] = (blk * lax.rsqrt(ms + EPS) * g_ref[:, sl]).astype(o_ref.dtype)
    elif mode == "glu":
        z = refs[2][...].astype(F32)
        o_ref[...] = (z * _sigmoid(acc)).astype(o_ref.dtype)
    elif mode == "residual":
        o_ref[...] = (refs[2][...] + acc).astype(o_ref.dtype)
    else:
        raise ValueError(mode)


def _mm(a, w, *, mode, tm, tn, out_dtype, extra=None, extra_kind=None):
    m, k = a.shape
    n = w.shape[1]
    tn = min(tn, n)
    in_specs = [pl.BlockSpec((tm, k), lambda i, j: (i, 0)),
                pl.BlockSpec((k, tn), lambda i, j: (0, j))]
    args = [a, w]
    if extra is not None:
        if extra_kind == "row":
            in_specs.append(pl.BlockSpec((1, tn), lambda i, j: (0, j)))
        else:
            in_specs.append(pl.BlockSpec((tm, tn), lambda i, j: (i, j)))
        args.append(extra)
    return pl.pallas_call(
        functools.partial(_mm_body, mode=mode),
        grid=(m // tm, n // tn),
        in_specs=in_specs,
        out_specs=pl.BlockSpec((tm, tn), lambda i, j: (i, j)),
        out_shape=jax.ShapeDtypeStruct((m, n), out_dtype),
        compiler_params=_cparams(("parallel", "arbitrary")),
        name="mm_" + mode,
    )(*args)


def _mm_mix(a1, w1, a2, w2, ga, gb, *, tm, tn):
    m, k1 = a1.shape
    k2 = a2.shape[1]
    n = w1.shape[1]
    return pl.pallas_call(
        functools.partial(_mm_body, mode="mix"),
        grid=(m // tm, n // tn),
        in_specs=[pl.BlockSpec((tm, k1), lambda i, j: (i, 0)),
                  pl.BlockSpec((k1, tn), lambda i, j: (0, j)),
                  pl.BlockSpec((tm, k2), lambda i, j: (i, 0)),
                  pl.BlockSpec((k2, tn), lambda i, j: (0, j)),
                  pl.BlockSpec((tm, tn), lambda i, j: (i, j)),
                  pl.BlockSpec((tm, tn), lambda i, j: (i, j))],
        out_specs=pl.BlockSpec((tm, tn), lambda i, j: (i, j)),
        out_shape=jax.ShapeDtypeStruct((m, n), BF16),
        compiler_params=_cparams(("parallel", "arbitrary")),
        name="mm_mix",
    )(a1, w1, a2, w2, ga, gb)


def _split3(x):
    hi = x.astype(BF16)
    r = x - hi.astype(F32)
    mid = r.astype(BF16)
    lo = (r - mid.astype(F32)).astype(BF16)
    return hi, mid, lo


def _forget_prep_body(f_ref, init_ref, eq_ref, ek_ref, oq_ref, ok_ref,
                      c_ref, cqa_ref, cka_ref, carry_ref):
    j = pl.program_id(1)

    @pl.when(j == 0)
    def _():
        carry_ref[...] = jnp.broadcast_to(init_ref[...], carry_ref.shape)

    f = f_ref[...]
    tb = f.shape[0]
    row = lax.broadcasted_iota(jnp.int32, (tb, tb), 0)
    col = lax.broadcasted_iota(jnp.int32, (tb, tb), 1)
    tri = (row >= col).astype(BF16)
    hi, mid, lo = _split3(f)
    c = (jnp.dot(tri, hi, preferred_element_type=F32)
         + jnp.dot(tri, mid, preferred_element_type=F32)
         + jnp.dot(tri, lo, preferred_element_type=F32)) + carry_ref[0:1, :]
    carry_ref[...] = jnp.broadcast_to(c[tb - 1:tb, :], carry_ref.shape)
    c_ref[...] = c
    pieces = jnp.concatenate(_split3(c), axis=1)
    cqa_ref[...] = (jnp.dot(pieces, eq_ref[...], preferred_element_type=F32)
                    + oq_ref[...]).astype(BF16)
    cka_ref[...] = (jnp.dot(pieces, ek_ref[...], preferred_element_type=F32)
                    + ok_ref[...]).astype(BF16)


def _forget_consts(n_heads):
    w = n_heads * HEAD_DIM
    eq = np.zeros((3 * LANES, w), np.float32)
    ek = np.zeros((3 * LANES, w), np.float32)
    oq = np.zeros((1, w), np.float32)
    ok = np.zeros((1, w), np.float32)
    for h in range(n_heads):
        for p in range(3):
            eq[p * LANES + h, h * HEAD_DIM + p] = 1.0
            ek[p * LANES + h, h * HEAD_DIM + 3 + p] = -1.0
            oq[0, h * HEAD_DIM + 3 + p] = 1.0
            ok[0, h * HEAD_DIM + p] = 1.0
    return (jnp.asarray(eq, BF16), jnp.asarray(ek, BF16), jnp.asarray(oq), jnp.asarray(ok))


def _forget_prep(f, init, n_batch, n_heads, tb):
    rows = f.shape[0]
    nb = rows // n_batch // tb
    w = n_heads * HEAD_DIM
    eq, ek, oq, ok = _forget_consts(n_heads)
    const = lambda shape: pl.BlockSpec(shape, lambda b, j: (0, 0))
    return pl.pallas_call(
        _forget_prep_body,
        grid=(n_batch, nb),
        in_specs=[pl.BlockSpec((tb, LANES), lambda b, j: (b * nb + j, 0)),
                  const((1, LANES)), const((3 * LANES, w)), const((3 * LANES, w)),
                  const((1, w)), const((1, w))],
        out_specs=[pl.BlockSpec((tb, LANES), lambda b, j: (b * nb + j, 0)),
                   pl.BlockSpec((tb, w), lambda b, j: (b * nb + j, 0)),
                   pl.BlockSpec((tb, w), lambda b, j: (b * nb + j, 0))],
        out_shape=[jax.ShapeDtypeStruct((rows, LANES), F32),
                   jax.ShapeDtypeStruct((rows, w), BF16),
                   jax.ShapeDtypeStruct((rows, w), BF16)],
        scratch_shapes=[pltpu.VMEM((8, LANES), F32)],
        compiler_params=_cparams(("parallel", "arbitrary")),
        name="forget_prep",
    )(f, init, eq, ek, oq, ok)


def _attn_body(q_ref, cqa_ref, k_ref, cka_ref, v_ref, km_ref, ckam_ref, vm_ref, o_ref, *, tq):
    qi = pl.program_id(2)
    nt = (((1,), (1,)), ((), ()))
    qa = jnp.concatenate([q_ref[...], cqa_ref[...]], axis=1)

    kma = jnp.concatenate([km_ref[...], ckam_ref[...]], axis=1)
    s = lax.dot_general(qa, kma, nt, preferred_element_type=F32)
    m = jnp.max(s, axis=1, keepdims=True)
    p = jnp.exp(s - m)
    l = jnp.sum(p, axis=1, keepdims=True)
    acc = jnp.dot(p.astype(BF16), vm_ref[...], preferred_element_type=F32)

    def step(j, carry, masked):
        m, l, acc = carry
        ks = pl.multiple_of(j * tq, tq)
        ka = jnp.concatenate([k_ref[pl.ds(ks, tq), :], cka_ref[pl.ds(ks, tq), :]], axis=1)
        s = lax.dot_general(qa, ka, nt, preferred_element_type=F32)
        if masked:
            row = lax.broadcasted_iota(jnp.int32, s.shape, 0)
            col = lax.broadcasted_iota(jnp.int32, s.shape, 1)
            s = jnp.where(row >= col, s, NEG_BIG)
        m_new = jnp.maximum(m, jnp.max(s, axis=1, keepdims=True))
        alpha = jnp.exp(m - m_new)
        p = jnp.exp(s - m_new)
        l = alpha * l + jnp.sum(p, axis=1, keepdims=True)
        acc = alpha * acc + jnp.dot(p.astype(BF16), v_ref[pl.ds(ks, tq), :],
                                    preferred_element_type=F32)
        return m_new, l, acc

    carry = lax.fori_loop(0, qi, lambda j, c: step(j, c, False), (m, l, acc))
    m, l, acc = step(qi, carry, True)
    o_ref[...] = (acc / l).astype(o_ref.dtype)


def _attention(q, cqa, k, cka, v, km, ckam, vm, *, n_batch, seq, n_heads, tq):
    nq = seq // tq
    hd = HEAD_DIM
    qspec = pl.BlockSpec((tq, hd), lambda b, h, i: (b * nq + i, h))
    kspec = pl.BlockSpec((seq, hd), lambda b, h, i: (b, h))
    mspec = pl.BlockSpec((LANES, hd), lambda b, h, i: (0, h))
    return pl.pallas_call(
        functools.partial(_attn_body, tq=tq),
        grid=(n_batch, n_heads, nq),
        in_specs=[qspec, qspec, kspec, kspec, kspec, mspec, mspec, mspec],
        out_specs=qspec,
        out_shape=jax.ShapeDtypeStruct(q.shape, BF16),
        compiler_params=_cparams(("parallel", "parallel", "arbitrary")),
        name="fox_attention",
    )(q, cqa, k, cka, v, km, ckam, vm)


def _s5_body(u_ref, m_ref, p_ref, ps_ref, q_ref, lam_ref, d_ref, z_ref,
             xin_ref, inp_ref, inps_ref, *, n_chunks):
    u = u_ref[0]
    inp_ref[...] = jnp.dot(u, p_ref[0], preferred_element_type=F32)
    inps_ref[...] = jnp.dot(u, ps_ref[0], preferred_element_type=F32)
    bp = SSM_BATCH_PAD
    a1 = jnp.broadcast_to(lam_ref[0, 0:1, :], (bp, LANES))
    a2 = jnp.broadcast_to(lam_ref[0, 1:2, :], (bp, LANES))
    a3 = jnp.broadcast_to(lam_ref[0, 2:3, :], (bp, LANES))

    def step(j, carry):
        x, xs = carry
        r = pl.multiple_of(j * bp, bp)
        xin_ref[pl.ds(r, bp), :] = x
        xn = a1 * x + a2 * xs + inp_ref[pl.ds(r, bp), :]
        xsn = a1 * xs + a3 * x + inps_ref[pl.ds(r, bp), :]
        return xn, xsn

    zero = jnp.zeros((bp, LANES), F32)
    lax.fori_loop(0, n_chunks, step, (zero, zero))
    y = (jnp.dot(u, m_ref[0], preferred_element_type=F32)
         + jnp.dot(xin_ref[...].astype(BF16), q_ref[0], preferred_element_type=F32)
         + d_ref[0] * u.astype(F32))
    z_ref[0] = _gelu_tanh(y).astype(z_ref.dtype)


def _s5_scan(u_g, m_op, p_op, ps_op, q_op, lam, dvec, n_chunks):
    g, rows, cw = u_g.shape
    sw = 2 * SSM_STATE
    gspec = lambda shape: pl.BlockSpec((1,) + shape, lambda i: (i, 0, 0))
    return pl.pallas_call(
        functools.partial(_s5_body, n_chunks=n_chunks),
        grid=(g,),
        in_specs=[gspec((rows, cw)), gspec((cw, cw)), gspec((cw, sw)), gspec((cw, sw)),
                  gspec((sw, cw)), gspec((8, sw)), gspec((1, cw))],
        out_specs=gspec((rows, cw)),
        out_shape=jax.ShapeDtypeStruct((g, rows, cw), BF16),
        scratch_shapes=[pltpu.VMEM((rows, sw), F32), pltpu.VMEM((rows, sw), F32),
                        pltpu.VMEM((rows, sw), F32)],
        compiler_params=_cparams(("parallel",)),
        name="s5_scan",
    )(u_g, m_op, p_op, ps_op, q_op, lam, dvec)


def _s5_operators(lam_re, lam_im, log_dt, b_re, b_im, c_re, c_im, d_skip):
    lc = SSM_CHUNK
    hp = lax.Precision.HIGHEST
    dt = jnp.exp(log_dt.astype(F32))[:, None]
    lr = lam_re.astype(F32)
    li = lam_im.astype(F32)
    taus = jnp.arange(lc + 1, dtype=F32)[:, None, None]
    mag = jnp.exp(lr * dt * taus)
    pr = mag * jnp.cos(li * dt * taus)
    pi = mag * jnp.sin(li * dt * taus)
    ar, ai = pr[1], pi[1]
    nr = ar - 1.0
    den = lr * lr + li * li
    fr = (nr * lr + ai * li) / den
    fi = (ai * lr - nr * li) / den
    br = b_re.astype(F32)
    bi = b_im.astype(F32)
    bbr = fr[..., None] * br - fi[..., None] * bi
    bbi = fr[..., None] * bi + fi[..., None] * br
    wr = pr[:lc, :, :, None] * bbr - pi[:lc, :, :, None] * bbi
    wi = pr[:lc, :, :, None] * bbi + pi[:lc, :, :, None] * bbr
    cr = c_re.astype(F32)
    ci = c_im.astype(F32)
    kern = (jnp.einsum("gop,tgpc->tgoc", cr, wr, precision=hp)
            - jnp.einsum("gop,tgpc->tgoc", ci, wi, precision=hp))
    s_idx = jnp.arange(lc)[:, None]
    t_idx = jnp.arange(lc)[None, :]
    lag = jnp.clip(t_idx - s_idx, 0, lc - 1)
    mk = jnp.where((t_idx >= s_idx)[:, :, None, None, None], kern[lag], 0.0)
    g = lr.shape[0]
    cw = lc * SSM_GROUP_CH
    m_op = jnp.transpose(mk, (2, 0, 4, 1, 3)).reshape(g, cw, cw)
    wr_rev = jnp.transpose(wr[::-1], (1, 0, 3, 2)).reshape(g, cw, SSM_STATE)
    wi_rev = jnp.transpose(wi[::-1], (1, 0, 3, 2)).reshape(g, cw, SSM_STATE)
    p_op = jnp.concatenate([wr_rev, wi_rev], axis=-1)
    ps_op = jnp.concatenate([wi_rev, wr_rev], axis=-1)
    pr1 = pr[1:lc + 1]
    pi1 = pi[1:lc + 1]
    q_re = cr[None] * pr1[:, :, None, :] - ci[None] * pi1[:, :, None, :]
    q_im = -(cr[None] * pi1[:, :, None, :] + ci[None] * pr1[:, :, None, :])
    q_op = jnp.concatenate([jnp.transpose(q_re, (1, 3, 0, 2)).reshape(g, SSM_STATE, cw),
                            jnp.transpose(q_im, (1, 3, 0, 2)).reshape(g, SSM_STATE, cw)], axis=1)
    al, bl = pr[lc], pi[lc]
    lam = jnp.stack([jnp.concatenate([al, al], -1), jnp.concatenate([-bl, bl], -1),
                     jnp.concatenate([bl, -bl], -1)], axis=1)
    lam = jnp.pad(lam, ((0, 0), (0, 5), (0, 0)))
    dvec = jnp.tile(d_skip.astype(F32), (1, lc)).reshape(g, 1, cw)
    return (m_op.astype(BF16), p_op.astype(BF16), ps_op.astype(BF16), q_op.astype(BF16), lam, dvec)


def _peer_scores_body(a_ref, w_ref, sk_ref, o_ref):
    acc = jnp.dot(a_ref[...], w_ref[...], preferred_element_type=F32)
    nt = (((1,), (1,)), ((), ()))
    for gi in range(acc.shape[1] // PEER_HALF):
        qg = acc[:, gi * PEER_HALF:(gi + 1) * PEER_HALF].astype(BF16)
        o_ref[gi] = lax.dot_general(sk_ref[gi % 2], qg, nt, preferred_element_type=F32)


def _peer_scores(a, w, sk, *, tm, tn):
    m, k = a.shape
    n = w.shape[1]
    gpt = tn // PEER_HALF
    return pl.pallas_call(
        _peer_scores_body,
        grid=(m // tm, n // tn),
        in_specs=[pl.BlockSpec((tm, k), lambda i, j: (i, 0)),
                  pl.BlockSpec((k, tn), lambda i, j: (0, j)),
                  pl.BlockSpec((2, PEER_KEYS, PEER_HALF), lambda i, j: (0, 0, 0))],
        out_specs=pl.BlockSpec((gpt, PEER_KEYS, tm), lambda i, j: (j, 0, i)),
        out_shape=jax.ShapeDtypeStruct((n // PEER_HALF, PEER_KEYS, m), F32),
        compiler_params=_cparams(("parallel", "arbitrary")),
        name="peer_scores",
    )(a, w, sk)


def _top_values(vals, k):
    n = vals.shape[0]
    idx = lax.broadcasted_iota(jnp.int32, vals.shape, 0)
    tops = []
    for r in range(k):
        mx = jnp.max(vals, axis=0, keepdims=True)
        tops.append(mx)
        if r + 1 < k:
            first = jnp.min(jnp.where(vals == mx, idx, n), axis=0, keepdims=True)
            vals = jnp.where(idx == first, -jnp.inf, vals)
    return tops


def _peer_route_body(s_ref, g_ref, tau_ref):
    def head(h, carry):
        s1 = s_ref[2 * h]
        s2 = s_ref[2 * h + 1]
        t1 = _top_values(s1, PEER_TOPK)
        t2 = _top_values(s2, PEER_TOPK)
        sv2 = jnp.concatenate(t2, axis=0)
        cand = jnp.concatenate([t1[a] + sv2 for a in range(PEER_TOPK)], axis=0)
        tc = _top_values(cand, PEER_TOPK)
        z = jnp.zeros_like(tc[0])
        for r in range(PEER_TOPK):
            z = z + jnp.exp(tc[r] - tc[0])
        g_ref[2 * h] = jnp.exp(s1 - t1[0]) / z
        g_ref[2 * h + 1] = jnp.exp(s2 - t2[0])
        tau_ref[pl.ds(h, 1), :] = tc[PEER_TOPK - 1]
        return carry

    lax.fori_loop(0, PEER_HEADS, head, 0)


def _peer_route(s_t, *, tr):
    ng, nk, t = s_t.shape
    return pl.pallas_call(
        _peer_route_body,
        grid=(t // tr,),
        in_specs=[pl.BlockSpec((ng, nk, tr), lambda i: (0, 0, i))],
        out_specs=[pl.BlockSpec((ng, nk, tr), lambda i: (0, 0, i)),
                   pl.BlockSpec((PEER_HEADS, tr), lambda i: (0, i))],
        out_shape=[jax.ShapeDtypeStruct((ng, nk, t), F32),
                   jax.ShapeDtypeStruct((PEER_HEADS, t), F32)],
        compiler_params=_cparams(("parallel",)),
        name="peer_route",
    )(s_t)


def _peer_dense_body(x_ref, u_ref, v_ref, s_ref, g_ref, tau_ref, o_ref, *, dn):
    j = pl.program_id(1)
    te = u_ref.shape[0]
    e1_per_tile = te // PEER_KEYS

    @pl.when(j == 0)
    def _():
        o_ref[...] = jnp.zeros_like(o_ref)

    nt = (((1,), (1,)), ((), ()))
    at = lax.dot_general(u_ref[...], x_ref[...], nt, preferred_element_type=F32)
    parts = []
    for r in range(e1_per_tile):
        e1 = j * e1_per_tile + r
        w = jnp.zeros((PEER_KEYS, at.shape[1]), F32)
        for h in range(PEER_HEADS):
            s1row = s_ref[2 * h, pl.ds(e1, 1), :]
            g1row = g_ref[2 * h, pl.ds(e1, 1), :]
            c = s1row + s_ref[2 * h + 1]
            w = w + jnp.where(c >= tau_ref[h:h + 1, :], g1row * g_ref[2 * h + 1], 0.0)
        a = at[r * PEER_KEYS:(r + 1) * PEER_KEYS, :]
        parts.append((w * _gelu_tanh(a)).astype(BF16))
    wg = jnp.concatenate(parts, axis=0) if len(parts) > 1 else parts[0]
    tn_ = (((0,), (0,)), ((), ()))
    d = v_ref.shape[1]
    for c0 in range(0, d, dn):
        o_ref[:, c0:c0 + dn] += lax.dot_general(wg, v_ref[:, c0:c0 + dn], tn_,
                                                preferred_element_type=F32)


def _peer_dense(x, u, v, s_t, g_t, tau, *, tt, te, dn, vmem):
    t, d = x.shape
    ne = u.shape[0]
    ng, nk, _ = s_t.shape
    return pl.pallas_call(
        functools.partial(_peer_dense_body, dn=dn),
        grid=(t // tt, ne // te),
        in_specs=[pl.BlockSpec((tt, d), lambda i, j: (i, 0)),
                  pl.BlockSpec((te, d), lambda i, j: (j, 0)),
                  pl.BlockSpec((te, d), lambda i, j: (j, 0)),
                  pl.BlockSpec((ng, nk, tt), lambda i, j: (0, 0, i)),
                  pl.BlockSpec((ng, nk, tt), lambda i, j: (0, 0, i)),
                  pl.BlockSpec((PEER_HEADS, tt), lambda i, j: (0, i))],
        out_specs=pl.BlockSpec((tt, d), lambda i, j: (i, 0)),
        out_shape=jax.ShapeDtypeStruct((t, d), F32),
        compiler_params=_cparams(("parallel", "arbitrary"), vmem),
        name="peer_dense",
    )(x, u, v, s_t, g_t, tau)


def _add_body(a_ref, b_ref, o_ref):
    o_ref[...] = a_ref[...] + b_ref[...]


def _add(a, b, tm):
    m, d = a.shape
    spec = pl.BlockSpec((tm, d), lambda i: (i, 0))
    return pl.pallas_call(
        _add_body, grid=(m // tm,), in_specs=[spec, spec], out_specs=spec,
        out_shape=jax.ShapeDtypeStruct((m, d), a.dtype),
        compiler_params=_cparams(("parallel",)), name="residual_add",
    )(a, b)


def _tiles(t):
    return dict(tm=min(1024, t), tn=512, tq=min(256, t), tb=min(256, t),
                tr=min(256, t), tt=min(512, t), te=256, dn=1024, tnorm=min(256, t))


def _layer(x2d, meta, p, n_batch, seq):
    t, d = x2d.shape
    cfg = _tiles(seq)
    tm, tn = cfg["tm"], cfg["tn"]
    n_heads = p["b_forget"].shape[0]
    aw = n_heads * HEAD_DIM
    groups = p["lam_re"].shape[0]
    sw = groups * SSM_GROUP_CH
    w_in = p["w_in"]
    col_k, col_v, col_f = aw, 2 * aw, 3 * aw
    col_u = col_f + n_heads
    col_ga = col_u + sw
    col_gb = col_ga + d

    w_qk = w_in[:, :col_v].astype(BF16)
    w_vu = jnp.concatenate([w_in[:, col_v:col_f], w_in[:, col_u:col_ga]], axis=1).astype(BF16)
    w_g = w_in[:, col_ga:].astype(BF16)
    w_f = jnp.pad(w_in[:, col_f:col_u], ((0, 0), (0, LANES - n_heads))).astype(BF16)
    b_f = jnp.pad(p["b_forget"].astype(F32), (0, LANES - n_heads)).reshape(1, LANES)
    scale = 1.0 / math.sqrt(HEAD_DIM)
    g_qk = jnp.concatenate([jnp.tile(p["q_norm_g"].astype(F32) * scale, n_heads),
                            jnp.tile(p["k_norm_g"].astype(F32), n_heads)]).reshape(1, 2 * aw)

    hn = _rmsnorm(x2d, p["norm1_g"], cfg["tnorm"])
    hn_m = _rmsnorm(meta, p["norm1_g"], N_META)
    qk = _mm(hn, w_qk, mode="headnorm", tm=tm, tn=tn, out_dtype=BF16, extra=g_qk, extra_kind="row")
    vu = _mm(hn, w_vu, mode="plain", tm=tm, tn=tn, out_dtype=BF16)
    gates = _mm(hn, w_g, mode="sigmoid", tm=tm, tn=tn, out_dtype=BF16)
    logf = _mm(hn, w_f, mode="logsig_bias", tm=tm, tn=LANES, out_dtype=F32, extra=b_f, extra_kind="row")
    qk_m = _mm(hn_m, w_qk, mode="headnorm", tm=N_META, tn=tn, out_dtype=BF16, extra=g_qk, extra_kind="row")
    vu_m = _mm(hn_m, w_vu, mode="plain", tm=N_META, tn=tn, out_dtype=BF16)
    logf_m = _mm(hn_m, w_f, mode="logsig_bias", tm=N_META, tn=LANES, out_dtype=F32, extra=b_f, extra_kind="row")

    zero_init = jnp.zeros((1, LANES), F32)
    c_m, _, cka_m = _forget_prep(logf_m, zero_init, 1, n_heads, N_META)
    _, cqa, cka = _forget_prep(logf, c_m[N_META - 1:N_META], n_batch, n_heads, cfg["tb"])
    pad_rows = LANES - N_META
    pad_bias = np.zeros((pad_rows, aw), np.float32)
    pad_bias[:, 3::HEAD_DIM] = NEG_BIG
    cka_mp = jnp.concatenate([cka_m, jnp.asarray(pad_bias, BF16)], axis=0)
    k_mp = jnp.pad(qk_m[:, aw:], ((0, pad_rows), (0, 0)))
    v_mp = jnp.pad(vu_m[:, :aw], ((0, pad_rows), (0, 0)))
    attn = _attention(qk[:, :aw], cqa, qk[:, aw:], cka, vu[:, :aw], k_mp, cka_mp, v_mp,
                      n_batch=n_batch, seq=seq, n_heads=n_heads, tq=cfg["tq"])

    lc = SSM_CHUNK
    n_main = seq // lc
    n_chunks = n_main + N_META // lc
    u_main = vu[:, aw:].reshape(n_batch, n_main, lc, groups, SSM_GROUP_CH)
    u_meta = jnp.broadcast_to(vu_m[:, aw:].reshape(1, N_META // lc, lc, groups, SSM_GROUP_CH),
                              (n_batch, N_META // lc, lc, groups, SSM_GROUP_CH))
    u_all = jnp.concatenate([u_meta, u_main], axis=1)
    u_g = jnp.transpose(u_all, (3, 1, 0, 2, 4))
    u_g = jnp.pad(u_g, ((0, 0), (0, 0), (0, SSM_BATCH_PAD - n_batch), (0, 0), (0, 0)))
    u_g = u_g.reshape(groups, n_chunks * SSM_BATCH_PAD, lc * SSM_GROUP_CH)
    ops = _s5_operators(p["lam_re"], p["lam_im"], p["log_dt"], p["b_re"], p["b_im"],
                        p["c_re"], p["c_im"], p["d_skip"])
    z_g = _s5_scan(u_g, *ops, n_chunks)
    z_g = z_g.reshape(groups, n_chunks, SSM_BATCH_PAD, lc, SSM_GROUP_CH)[:, N_META // lc:, :n_batch]
    z = jnp.transpose(z_g, (2, 1, 3, 0, 4)).reshape(t, sw)
    ssm = _mm(z, p["w_glu"].astype(BF16), mode="glu", tm=tm, tn=tn, out_dtype=BF16,
              extra=z, extra_kind="tile")

    mix = _mm_mix(attn, p["w_branch_attn"].astype(BF16), ssm, p["w_branch_ssm"].astype(BF16),
                  gates[:, :d], gates[:, d:], tm=tm, tn=tn)
    h1 = _mm(mix, p["w_out"].astype(BF16), mode="residual", tm=tm, tn=tn, out_dtype=F32,
             extra=x2d, extra_kind="tile")

    hn2 = _rmsnorm(h1, p["norm2_g"], cfg["tnorm"])
    s_t = _peer_scores(hn2, p["w_query"].astype(BF16), p["sub_keys"].astype(BF16), tm=tm, tn=tn)
    g_t, tau = _peer_route(s_t, tr=cfg["tr"])
    y = _peer_dense(hn2, p["expert_u"].astype(BF16), p["expert_v"].astype(BF16), s_t, g_t, tau,
                    tt=cfg["tt"], te=cfg["te"], dn=cfg["dn"], vmem=56 * 1024 * 1024)
    return _add(h1, y, cfg["tnorm"])


def kernel(x, meta_tokens, norm1_g, w_in, b_forget, q_norm_g, k_norm_g, lam_re, lam_im, log_dt,
           b_re, b_im, c_re, c_im, d_skip, w_glu, w_branch_attn, w_branch_ssm, w_out, norm2_g,
           w_query, sub_keys, expert_u, expert_v):
    n_batch, seq, d = x.shape
    names = ("norm1_g", "w_in", "b_forget", "q_norm_g", "k_norm_g", "lam_re", "lam_im", "log_dt",
             "b_re", "b_im", "c_re", "c_im", "d_skip", "w_glu", "w_branch_attn", "w_branch_ssm",
             "w_out", "norm2_g", "w_query", "sub_keys", "expert_u", "expert_v")
    vals = (norm1_g, w_in, b_forget, q_norm_g, k_norm_g, lam_re, lam_im, log_dt, b_re, b_im, c_re,
            c_im, d_skip, w_glu, w_branch_attn, w_branch_ssm, w_out, norm2_g, w_query, sub_keys,
            expert_u, expert_v)
    depth = norm1_g.shape[0]
    assert depth == 1, "meta-token rows are only carried through a single layer"
    h = x.reshape(n_batch * seq, d)
    meta = meta_tokens.astype(x.dtype)
    for layer in range(depth):
        p = {k: v[layer] for k, v in zip(names, vals)}
        h = _layer(h, meta, p, n_batch, seq)
    return h.reshape(n_batch, seq, d)
```
